```python
import jax, jax.numpy as jnp
from jax import lax
import numpy as np

D_MODEL = 2048
BATCH = 4
SEQ = 4096
DEPTH = 4
DEC_BATCH = 2
DEC_SEQ = 8192
PAST_LEN = 128

N_MIXERS = 2
N_HGRN_LAYERS = (DEPTH + 1) // 2
N_CONV_LAYERS = DEPTH // 2
HGRN_HEADS = 16
HGRN_DK = 128
HGRN_DV = D_MODEL // HGRN_HEADS
FORGET_DIM = HGRN_HEADS * HGRN_DK
HGRN_IN = 3 * FORGET_DIM + 2 * D_MODEL
CHUNK = 64
CONV_WIDTH = 3
N_EXPERTS = 16
EXPERT_FF = 2048
CAPACITY_FACTOR = 2
EPS = 1e-6

kernel_name = "hybrid_hgrn2_shortconv_ec_moe_encoder"


def rmsnorm(x, gain):
    xf = x.astype(jnp.float32)
    y = xf * lax.rsqrt(jnp.mean(xf * xf, axis=-1, keepdims=True) + EPS)
    return (y * gain.astype(jnp.float32)).astype(x.dtype)


def gated_recurrence(q, v, f_pre, lb):
    bsz, seq, heads, _ = q.shape
    nc = seq // CHUNK
    log_f = jnp.log(lb + (1.0 - lb) * jax.nn.sigmoid(f_pre))
    k = (1.0 - lb) * jax.nn.sigmoid(-f_pre)
    rs = lambda t: t.reshape(bsz, nc, CHUNK, heads, t.shape[-1])
    q, k, v, log_f = rs(q), rs(k), rs(v), rs(log_f)
    cum = jnp.cumsum(log_f, axis=2)
    ref = cum[:, :, CHUNK // 2 - 1][:, :, None]
    scores = jnp.einsum('bnthd,bnshd->bnhts', q * jnp.exp(cum - ref), k * jnp.exp(ref - cum))
    mask = jnp.tril(jnp.ones((CHUNK, CHUNK), dtype=bool))
    scores = jnp.where(mask, scores, 0.0)
    o_intra = jnp.einsum('bnhts,bnshv->bnthv', scores, v)
    cum_last = cum[:, :, -1]
    q_dec = q * jnp.exp(cum)
    k_dec = k * jnp.exp(cum_last[:, :, None] - cum)

    def step(state, inp):
        qd, kd, vv, cl = inp
        o = jnp.einsum('bthd,bhdv->bthv', qd, state)
        state = jnp.exp(cl)[..., None] * state + jnp.einsum('bthd,bthv->bhdv', kd, vv)
        return state, o

    init = jnp.zeros((bsz, heads, HGRN_DK, v.shape[-1]), jnp.float32)
    xs = (jnp.moveaxis(q_dec, 1, 0), jnp.moveaxis(k_dec, 1, 0), jnp.moveaxis(v, 1, 0), jnp.moveaxis(cum_last, 1, 0))
    _, o_inter = lax.scan(step, init, xs)
    o = o_intra + jnp.moveaxis(o_inter, 0, 1)
    return o.reshape(bsz, seq, heads, v.shape[-1])


def hgrn_mixer(h, w_in, lb, onorm, w_out):
    bsz, seq, _ = h.shape
    proj = h @ w_in
    q, f_fwd, f_bwd, i, g = jnp.split(proj, [FORGET_DIM, 2 * FORGET_DIM, 3 * FORGET_DIM, 3 * FORGET_DIM + D_MODEL], axis=-1)
    q = jax.nn.silu(q.astype(jnp.float32)).reshape(bsz, seq, HGRN_HEADS, HGRN_DK)
    v = i.astype(jnp.float32).reshape(bsz, seq, HGRN_HEADS, HGRN_DV)
    f_fwd = f_fwd.astype(jnp.float32).reshape(bsz, seq, HGRN_HEADS, HGRN_DK)
    f_bwd = f_bwd.astype(jnp.float32).reshape(bsz, seq, HGRN_HEADS, HGRN_DK)
    lb = lb.reshape(2, HGRN_HEADS, HGRN_DK)
    o_fwd = gated_recurrence(q, v, f_fwd, lb[0])
    o_bwd = jnp.flip(gated_recurrence(jnp.flip(q, 1), jnp.flip(v, 1), jnp.flip(f_bwd, 1), lb[1]), 1)
    o = o_fwd + o_bwd
    o = o * lax.rsqrt(jnp.mean(o * o, axis=-1, keepdims=True) + EPS) * onorm.astype(jnp.float32).reshape(HGRN_HEADS, HGRN_DV)
    o = o.reshape(bsz, seq, D_MODEL) * jax.nn.silu(g.astype(jnp.float32))
    return o.astype(h.dtype) @ w_out


def shortconv_mixer(h, w_in, conv_w, w_out):
    proj = h @ w_in
    b_gate, c_gate, u = jnp.split(proj, 3, axis=-1)
    z = c_gate * u
    zp = jnp.pad(z, ((0, 0), (1, 1), (0, 0)))
    zc = zp[:, :-2] * conv_w[0] + zp[:, 1:-1] * conv_w[1] + zp[:, 2:] * conv_w[2]
    return (b_gate * zc) @ w_out


def ec_moe(h, router_w, w_gate, w_up, w_down):
    bsz, seq, d = h.shape
    n_tok = bsz * seq
    cap = max(1, CAPACITY_FACTOR * n_tok // N_EXPERTS)
    xf = h.reshape(n_tok, d)
    aff = jax.nn.softmax((xf @ router_w).astype(jnp.float32), axis=-1)
    gates, idx = lax.top_k(aff.T, cap)
    xe = xf[idx]
    hid = jax.nn.silu(jnp.einsum('ecd,edf->ecf', xe, w_gate)) * jnp.einsum('ecd,edf->ecf', xe, w_up)
    ye = jnp.einsum('ecf,efd->ecd', hid, w_down) * gates[..., None].astype(h.dtype)
    out = jnp.zeros((n_tok, d), h.dtype).at[idx.reshape(-1)].add(ye.reshape(-1, d))
    return out.reshape(bsz, seq, d)


def trunk(x, lbs, norm_mix, norm_ffn, norm_final, hgrn_w_in, hgrn_onorm, hgrn_w_out, conv_w_in, conv_w, conv_w_out, router_w, w_gate, w_up, w_down):
    for layer in range(DEPTH):
        j = layer // N_MIXERS
        h = rmsnorm(x, norm_mix[layer])
        if layer % N_MIXERS == 0:
            x = x + hgrn_mixer(h, hgrn_w_in[j], lbs[j], hgrn_onorm[j], hgrn_w_out[j])
        else:
            x = x + shortconv_mixer(h, conv_w_in[j], conv_w[j], conv_w_out[j])
        h = rmsnorm(x, norm_ffn[layer])
        x = x + ec_moe(h, router_w[layer], w_gate[layer], w_up[layer], w_down[layer])
    return rmsnorm(x, norm_final)


def setup_inputs(seed: int = 0) -> dict:
    key = jax.random.key(seed)
    ks = jax.random.split(key, 18)
    f32 = jnp.float32
    nrm = lambda k, shape, scale: jax.random.normal(k, shape, f32) * scale
    return {
        "x_prompt": nrm(ks[0], (BATCH, SEQ, D_MODEL), 1.0),
        "x_sample": nrm(ks[1], (DEC_BATCH, DEC_SEQ, D_MODEL), 1.0),
        "norm_mix": 1.0 + nrm(ks[2], (DEPTH, D_MODEL), 0.02),
        "norm_ffn": 1.0 + nrm(ks[3], (DEPTH, D_MODEL), 0.02),
        "norm_final": 1.0 + nrm(ks[4], (D_MODEL,), 0.02),
        "hgrn_w_in": nrm(ks[5], (N_HGRN_LAYERS, D_MODEL, HGRN_IN), D_MODEL ** -0.5),
        "hgrn_lb": nrm(ks[6], (N_HGRN_LAYERS, 2, FORGET_DIM), 0.5),
        "hgrn_onorm": 1.0 + nrm(ks[7], (N_HGRN_LAYERS, D_MODEL), 0.02),
        "hgrn_w_out": nrm(ks[8], (N_HGRN_LAYERS, D_MODEL, D_MODEL), D_MODEL ** -0.5),
        "conv_w_in": nrm(ks[9], (N_CONV_LAYERS, D_MODEL, 3 * D_MODEL), D_MODEL ** -0.5),
        "conv_w": nrm(ks[10], (N_CONV_LAYERS, CONV_WIDTH, D_MODEL), CONV_WIDTH ** -0.5),
        "conv_w_out": nrm(ks[11], (N_CONV_LAYERS, D_MODEL, D_MODEL), D_MODEL ** -0.5),
        "router_w": nrm(ks[12], (DEPTH, D_MODEL, N_EXPERTS), D_MODEL ** -0.5),
        "w_gate": nrm(ks[13], (DEPTH, N_EXPERTS, D_MODEL, EXPERT_FF), D_MODEL ** -0.5),
        "w_up": nrm(ks[14], (DEPTH, N_EXPERTS, D_MODEL, EXPERT_FF), D_MODEL ** -0.5),
        "w_down": nrm(ks[15], (DEPTH, N_EXPERTS, EXPERT_FF, D_MODEL), EXPERT_FF ** -0.5),
    }


def reference(x_prompt, x_sample, norm_mix, norm_ffn, norm_final, hgrn_w_in, hgrn_lb, hgrn_onorm, hgrn_w_out, conv_w_in, conv_w, conv_w_out, router_w, w_gate, w_up, w_down):
    lbs = jnp.cumsum(jax.nn.softmax(hgrn_lb.astype(jnp.float32), axis=0), axis=0)
    lbs = lbs - lbs[0:1]
    y_prompt = trunk(x_prompt, lbs, norm_mix, norm_ffn, norm_final, hgrn_w_in, hgrn_onorm, hgrn_w_out, conv_w_in, conv_w, conv_w_out, router_w, w_gate, w_up, w_down)
    y_sample = trunk(x_sample, lbs, norm_mix, norm_ffn, norm_final, hgrn_w_in, hgrn_onorm, hgrn_w_out, conv_w_in, conv_w, conv_w_out, router_w, w_gate, w_up, w_down)
    return (y_prompt, y_sample)
```

```python
import functools

import jax
import jax.numpy as jnp
from jax import lax
from jax.experimental import pallas as pl
from jax.experimental.pallas import tpu as pltpu

HGRN_HEADS = 16
HEAD_DIM = 128
CHUNK = 64
CAPACITY_FACTOR = 2
EPS = 1e-6

LANES = 128
SUBLANES = 8
VMEM_LIMIT = 56 * 1024 * 1024

F32 = jnp.float32
BF16 = jnp.bfloat16


def _cparams(sem):
    return pltpu.CompilerParams(dimension_semantics=sem, vmem_limit_bytes=VMEM_LIMIT)


def _rms_scale(x):
    return lax.rsqrt(jnp.mean(x * x, axis=-1, keepdims=True) + EPS)


def _split3(x):
    hi = x.astype(BF16)
    r1 = x - hi.astype(F32)
    mid = r1.astype(BF16)
    lo = (r1 - mid.astype(F32)).astype(BF16)
    return hi, mid, lo


def _dot(a, b):
    return jnp.dot(a, b, preferred_element_type=F32)


def _dot_nt(a, b):
    return lax.dot_general(a, b, (((1,), (1,)), ((), ())), preferred_element_type=F32)


def _dot_tn(a, b):
    return lax.dot_general(a, b, (((0,), (0,)), ((), ())), preferred_element_type=F32)


def _lbs_kernel(lb_ref, o_ref):
    n = lb_ref.shape[0]
    rows = [lb_ref[j] for j in range(n)]
    m = functools.reduce(jnp.maximum, rows)
    ex = [jnp.exp(r - m) for r in rows]
    tot = functools.reduce(lambda a, b: a + b, ex)
    sm = [e / tot for e in ex]
    acc = sm[0]
    o_ref[0] = acc - sm[0]
    for j in range(1, n):
        acc = acc + sm[j]
        o_ref[j] = acc - sm[0]


def _lower_bounds(hgrn_lb):
    return pl.pallas_call(
        _lbs_kernel, out_shape=jax.ShapeDtypeStruct(hgrn_lb.shape, F32), name="hgrn_lbs",
    )(hgrn_lb.astype(F32))


def _norm_matmul_kernel(x_ref, g_ref, w_ref, o_ref, h_ref):
    @pl.when(pl.program_id(1) == 0)
    def _():
        x = x_ref[...]
        h_ref[...] = (x * _rms_scale(x) * g_ref[...]).astype(BF16)

    o_ref[...] = _dot(h_ref[...], w_ref[...])


def _tile(n, pref, unit=LANES):
    t = min(pref, n) // unit * unit
    while n % t:
        t -= unit
    return t


def _norm_matmul(x, gain, w):
    m, k = x.shape
    n = w.shape[1]
    tm = _tile(m, 1024, SUBLANES)
    tn = _tile(n, 1024)
    return pl.pallas_call(
        _norm_matmul_kernel,
        out_shape=jax.ShapeDtypeStruct((m, n), F32),
        grid=(m // tm, n // tn),
        in_specs=[
            pl.BlockSpec((tm, k), lambda i, j: (i, 0)),
            pl.BlockSpec((1, k), lambda i, j: (0, 0)),
            pl.BlockSpec((k, tn), lambda i, j: (0, j)),
        ],
        out_specs=pl.BlockSpec((tm, tn), lambda i, j: (i, j)),
        scratch_shapes=[pltpu.VMEM((tm, k), BF16)],
        compiler_params=_cparams(("parallel", "arbitrary")),
        name="norm_matmul",
    )(x, gain.reshape(1, k), w)


def _matmul_res_kernel(a_ref, w_ref, r_ref, o_ref):
    o_ref[...] = r_ref[...] + _dot(a_ref[...], w_ref[...])


def _matmul_residual(a, w, res):
    m, k = a.shape
    n = w.shape[1]
    tm = min(512, m)
    return pl.pallas_call(
        _matmul_res_kernel,
        out_shape=jax.ShapeDtypeStruct((m, n), F32),
        grid=(m // tm,),
        in_specs=[
            pl.BlockSpec((tm, k), lambda i: (i, 0)),
            pl.BlockSpec((k, n), lambda i: (0, 0)),
            pl.BlockSpec((tm, n), lambda i: (i, 0)),
        ],
        out_specs=pl.BlockSpec((tm, n), lambda i: (i, 0)),
        compiler_params=_cparams(("parallel",)),
        name="matmul_residual",
    )(a, w, res)


def _conv_out_kernel(b_ref, c_ref, u_ref, cp_ref, up_ref, cn_ref, un_ref, cw_ref, w_ref, x_ref,
                     o_ref, *, tm, starts, ends):
    tok0 = pl.program_id(0) * tm
    is_start = functools.reduce(jnp.logical_or, [tok0 == s for s in starts])
    is_end = functools.reduce(jnp.logical_or, [tok0 + tm == e for e in ends])
    z = c_ref[...] * u_ref[...]
    z_before = cp_ref[SUBLANES - 1:SUBLANES, :] * up_ref[SUBLANES - 1:SUBLANES, :]
    z_after = cn_ref[0:1, :] * un_ref[0:1, :]
    z_before = jnp.where(is_start, 0.0, z_before)
    z_after = jnp.where(is_end, 0.0, z_after)
    row = lax.broadcasted_iota(jnp.int32, z.shape, 0)
    z_prev = jnp.where(row == 0, z_before, pltpu.roll(z, 1, axis=0))
    z_next = jnp.where(row == tm - 1, z_after, pltpu.roll(z, tm - 1, axis=0))
    zc = z_prev * cw_ref[0:1, :] + z * cw_ref[1:2, :] + z_next * cw_ref[2:3, :]
    a = (b_ref[...] * zc).astype(BF16)
    o_ref[...] = x_ref[...] + _dot(a, w_ref[...])


def _conv_out(proj, conv_w, w_out, x, starts, ends):
    m, d = x.shape
    tm = min(256, m)
    nsub = tm // SUBLANES
    last_sub = m // SUBLANES - 1
    kern = functools.partial(_conv_out_kernel, tm=tm, starts=starts, ends=ends)
    prev_map = lambda sec: (lambda i: (jnp.maximum(i * nsub - 1, 0), sec))
    next_map = lambda sec: (lambda i: (jnp.minimum((i + 1) * nsub, last_sub), sec))
    return pl.pallas_call(
        kern,
        out_shape=jax.ShapeDtypeStruct((m, d), F32),
        grid=(m // tm,),
        in_specs=[
            pl.BlockSpec((tm, d), lambda i: (i, 0)),
            pl.BlockSpec((tm, d), lambda i: (i, 1)),
            pl.BlockSpec((tm, d), lambda i: (i, 2)),
            pl.BlockSpec((SUBLANES, d), prev_map(1)),
            pl.BlockSpec((SUBLANES, d), prev_map(2)),
            pl.BlockSpec((SUBLANES, d), next_map(1)),
            pl.BlockSpec((SUBLANES, d), next_map(2)),
            pl.BlockSpec((3, d), lambda i: (0, 0)),
            pl.BlockSpec((d, d), lambda i: (0, 0)),
            pl.BlockSpec((tm, d), lambda i: (i, 0)),
        ],
        out_specs=pl.BlockSpec((tm, d), lambda i: (i, 0)),
        compiler_params=_cparams(("parallel",)),
        name="conv_out",
    )(proj, proj, proj, proj, proj, proj, proj, conv_w, w_out, x)


def _hgrn_kernel(*refs, reverse, tt, hb, bounds, final):
    if final:
        q_ref, f_ref, v_ref, lb_ref, of_ref, g_ref, on_ref, o_ref, s_ref = refs
    else:
        q_ref, f_ref, v_ref, lb_ref, o_ref, s_ref = refs
    i = pl.program_id(1)
    blk = (pl.num_programs(1) - 1 - i) if reverse else i
    edge = (blk + 1) * tt if reverse else blk * tt
    reset = functools.reduce(jnp.logical_or, [edge == b for b in bounds])

    @pl.when(reset)
    def _():
        s_ref[...] = jnp.zeros_like(s_ref)

    lb = lb_ref[...]
    fp = f_ref[...]
    logf = jnp.log(lb + (1.0 - lb) * jax.nn.sigmoid(fp))
    kk = (1.0 - lb) * jax.nn.sigmoid(-fp)
    qr = q_ref[...]
    q = qr * jax.nn.sigmoid(qr)

    r = lax.broadcasted_iota(jnp.int32, (tt, tt), 0)
    c = lax.broadcasted_iota(jnp.int32, (tt, tt), 1)
    same = (r // CHUNK) == (c // CHUNK)
    ordered = (c >= r) if reverse else (c <= r)
    tri = jnp.where(same & ordered, 1.0, 0.0).astype(BF16)
    l_hi, l_mid, l_lo = _split3(logf)
    cum = _dot(tri, l_hi) + _dot(tri, l_mid) + _dot(tri, l_lo)

    rc = lax.broadcasted_iota(jnp.int32, (CHUNK, CHUNK), 0)
    cc = lax.broadcasted_iota(jnp.int32, (CHUNK, CHUNK), 1)
    causal = (cc >= rc) if reverse else (cc <= rc)
    ref_row = CHUNK // 2 if reverse else CHUNK // 2 - 1
    last_row = 0 if reverse else CHUNK - 1

    nchunk = tt // CHUNK
    order = range(nchunk - 1, -1, -1) if reverse else range(nchunk)
    for h in range(hb):
        ls = slice(h * HEAD_DIM, (h + 1) * HEAD_DIM)
        st = s_ref[h]
        for ci in order:
            ts = slice(ci * CHUNK, (ci + 1) * CHUNK)
            cum_c = cum[ts, ls]
            q_c = q[ts, ls]
            k_c = kk[ts, ls]
            v_c = v_ref[ts, ls].astype(BF16)
            ref = cum_c[ref_row:ref_row + 1, :]
            cl = cum_c[last_row:last_row + 1, :]
            a = (q_c * jnp.exp(cum_c - ref)).astype(BF16)
            bk = (k_c * jnp.exp(ref - cum_c)).astype(BF16)
            scores = jnp.where(causal, _dot_nt(a, bk), 0.0)
            o = _dot(scores.astype(BF16), v_c)
            qd = (q_c * jnp.exp(cum_c)).astype(BF16)
            o = o + _dot_nt(qd, st.astype(BF16))
            kd = (k_c * jnp.exp(cl - cum_c)).astype(BF16)
            st = jnp.exp(cl) * st + _dot_tn(v_c, kd)
            if final:
                ot = of_ref[ts, ls] + o
                gr = g_ref[ts, ls]
                y = ot * _rms_scale(ot) * on_ref[:, ls] * (gr * jax.nn.sigmoid(gr))
                o_ref[ts, ls] = y.astype(o_ref.dtype)
            else:
                o_ref[ts, ls] = o
        s_ref[h] = st


def _hgrn_pass(proj, lb, o_fwd, onorm, starts, ends, reverse):
    m = proj.shape[0]
    d = proj.shape[1] // 5
    hb = min(4, HGRN_HEADS)
    wl = hb * HEAD_DIM
    tt = 256
    nblk = m // tt
    nh = d // wl
    final = reverse
    bmap = (lambda i: nblk - 1 - i) if reverse else (lambda i: i)
    col = lambda sec: (lambda h, i: (bmap(i), sec * nh + h))
    fsec = 2 if reverse else 1
    in_specs = [
        pl.BlockSpec((tt, wl), col(0)),
        pl.BlockSpec((tt, wl), col(fsec)),
        pl.BlockSpec((tt, wl), col(3)),
        pl.BlockSpec((1, wl), lambda h, i: (0, h)),
    ]
    args = [proj, proj, proj, lb.reshape(1, d)]
    if final:
        in_specs += [
            pl.BlockSpec((tt, wl), lambda h, i: (bmap(i), h)),
            pl.BlockSpec((tt, wl), col(4)),
            pl.BlockSpec((1, wl), lambda h, i: (0, h)),
        ]
        args += [o_fwd, proj, onorm.reshape(1, d)]
    kern = functools.partial(_hgrn_kernel, reverse=reverse, tt=tt, hb=hb,
                             bounds=ends if reverse else starts, final=final)
    return pl.pallas_call(
        kern,
        out_shape=jax.ShapeDtypeStruct((m, d), BF16 if final else F32),
        grid=(nh, nblk),
        in_specs=in_specs,
        out_specs=pl.BlockSpec((tt, wl), lambda h, i: (bmap(i), h)),
        scratch_shapes=[pltpu.VMEM((hb, HEAD_DIM, HEAD_DIM), F32)],
        compiler_params=_cparams(("parallel", "arbitrary")),
        name="hgrn_bwd" if reverse else "hgrn_fwd",
    )(*args)


def _router_kernel(x_ref, g_ref, rw_ref, aff_ref):
    x = x_ref[...]
    h = x * _rms_scale(x) * g_ref[...]
    logits = lax.dot_general(rw_ref[...], h, (((1,), (1,)), ((), ())),
                             precision=lax.Precision.HIGHEST, preferred_element_type=F32)
    mx = jnp.max(logits, axis=0, keepdims=True)
    p = jnp.exp(logits - mx)
    aff_ref[...] = p / jnp.sum(p, axis=0, keepdims=True)


def _router(x, gain, rw_t):
    m, d = x.shape
    e = rw_t.shape[0]
    tm = min(512, m)
    return pl.pallas_call(
        _router_kernel,
        out_shape=jax.ShapeDtypeStruct((e, m), F32),
        grid=(m // tm,),
        in_specs=[
            pl.BlockSpec((tm, d), lambda i: (i, 0)),
            pl.BlockSpec((1, d), lambda i: (0, 0)),
            pl.BlockSpec((e, d), lambda i: (0, 0)),
        ],
        out_specs=pl.BlockSpec((e, tm), lambda i: (0, i)),
        compiler_params=_cparams(("parallel",)),
        name="moe_router",
    )(x, gain.reshape(1, d), rw_t)


def _select_kernel(aff_ref, mask_ref, rank_ref, apos_ref, tokoff_ref, ktot_ref, bnd_ref,
                   *, n_exp, nb, cap):
    rows = n_exp * nb
    aff = aff_ref[...]
    b3 = pltpu.bitcast(aff, jnp.int32).reshape(n_exp, nb, LANES)

    def count3(m3):
        s = jnp.sum(jnp.where(m3, 1.0, 0.0), axis=1, keepdims=True)
        return jnp.sum(s, axis=2, keepdims=True)

    def body(i, t):
        cand = t | jnp.left_shift(jnp.int32(1), 30 - i)
        return jnp.where(count3(b3 >= cand) >= cap, cand, t)

    thr = lax.fori_loop(0, 31, body, jnp.zeros((n_exp, 1, 1), jnp.int32))
    gt3 = b3 > thr
    eq3 = b3 == thr
    need = cap - count3(gt3)

    lane_r = lax.broadcasted_iota(jnp.int32, (LANES, LANES), 0)
    lane_c = lax.broadcasted_iota(jnp.int32, (LANES, LANES), 1)
    upper = jnp.where(lane_r < lane_c, 1.0, 0.0).astype(BF16)
    ones = jnp.ones((LANES, LANES), BF16)
    br = lax.broadcasted_iota(jnp.int32, (nb, nb), 0)
    bc = lax.broadcasted_iota(jnp.int32, (nb, nb), 1)
    lower_nb = jnp.where(bc < br, 1.0, 0.0).astype(BF16)

    def excl_cumsum(m2):
        mb = m2.astype(BF16)
        within = _dot(mb, upper)
        rowtot = _dot(mb, ones)
        rt = rowtot.astype(BF16)
        rowoff = jnp.concatenate(
            [_dot(lower_nb, rt[e * nb:(e + 1) * nb]) for e in range(n_exp)], axis=0)
        return within + rowoff, rowoff, rowtot

    eq2 = jnp.where(eq3, 1.0, 0.0).reshape(rows, LANES)
    eq_rank, _, _ = excl_cumsum(eq2)
    take_eq = eq3 & (eq_rank.reshape(n_exp, nb, LANES) < need)
    m3 = jnp.where(gt3 | take_eq, 1.0, 0.0)
    m2 = m3.reshape(rows, LANES)
    rank, rowoff, rowtot = excl_cumsum(m2)
    mask_ref[...] = m2
    rank_ref[...] = rank

    kbefore = []
    run = jnp.zeros((nb, LANES), F32)
    for e in range(n_exp):
        kbefore.append(run)
        run = run + m3[e]
    ktot = run
    ktot_ref[...] = ktot
    kb = ktot.astype(BF16)
    t_within = _dot(kb, upper)
    t_rowtot = _dot(kb, ones)
    t_hi = jnp.floor(t_rowtot * (1.0 / 64.0))
    t_lo = t_rowtot - 64.0 * t_hi
    t_rowoff = 64.0 * _dot(lower_nb, t_hi.astype(BF16)) + _dot(lower_nb, t_lo.astype(BF16))
    tokoff = t_within + t_rowoff
    tokoff_ref[...] = tokoff
    for e in range(n_exp):
        apos_ref[e * nb:(e + 1) * nb, :] = tokoff + kbefore[e]

    ntile = cap // LANES
    lane = lax.broadcasted_iota(jnp.int32, (n_exp, 1, LANES), 2)
    ro3 = rowoff.reshape(n_exp, nb, LANES)
    re3 = ro3 + rowtot.reshape(n_exp, nb, LANES)
    acc = jnp.zeros((n_exp, 1, LANES), F32)
    for p in range(ntile):
        lo = jnp.sum(jnp.where(re3 <= p * LANES, 1.0, 0.0), axis=1, keepdims=True)
        hi = jnp.sum(jnp.where(ro3 < (p + 1) * LANES, 1.0, 0.0), axis=1, keepdims=True)
        acc = jnp.where(lane == p, lo, acc)
        acc = jnp.where(lane == p + LANES // 2, hi, acc)
    bnd_ref[...] = acc.reshape(n_exp, LANES).astype(jnp.int32)


def _select(aff_g, cap, n_exp):
    g, rows, _ = aff_g.shape
    nb = rows // n_exp
    assert cap % LANES == 0 and cap // LANES <= LANES // 2 and nb % SUBLANES == 0
    kern = functools.partial(_select_kernel, n_exp=n_exp, nb=nb, cap=cap)
    big = pl.BlockSpec((None, rows, LANES), lambda i: (i, 0, 0))
    small = pl.BlockSpec((None, nb, LANES), lambda i: (i, 0, 0))
    return pl.pallas_call(
        kern,
        out_shape=(
            jax.ShapeDtypeStruct((g, rows, LANES), F32),
            jax.ShapeDtypeStruct((g, rows, LANES), F32),
            jax.ShapeDtypeStruct((g, rows, LANES), F32),
            jax.ShapeDtypeStruct((g, nb, LANES), F32),
            jax.ShapeDtypeStruct((g, nb, LANES), F32),
            jax.ShapeDtypeStruct((g, n_exp, LANES), jnp.int32),
        ),
        grid=(g,),
        in_specs=[big],
        out_specs=(big, big, big, small, small,
                   pl.BlockSpec((None, n_exp, LANES), lambda i: (i, 0, 0))),
        compiler_params=_cparams(("parallel",)),
        name="moe_select",
    )(aff_g)


def _compact_kernel(bnd_ref, mask_ref, rank_ref, apos_ref, aff_ref, o_ref, acc_ref,
                    *, n_exp, ntile, ng, ec):
    g = pl.program_id(0)
    e = pl.program_id(1)
    p = pl.program_id(2)
    base = (g * n_exp + e) * LANES
    lo = bnd_ref[base + p]
    hi = bnd_ref[base + LANES // 2 + p]
    slot = lax.broadcasted_iota(jnp.int32, (LANES, LANES), 0) + p * LANES
    lane = lax.broadcasted_iota(jnp.int32, (1, LANES), 1).astype(F32)
    acc_ref[...] = jnp.zeros_like(acc_ref)

    def body(r, carry):
        rank_r = rank_ref[pl.ds(r, 1), :]
        mask_r = mask_ref[pl.ds(r, 1), :]
        onehot = jnp.where((rank_r.astype(jnp.int32) == slot) & (mask_r > 0.5), 1.0, 0.0).astype(BF16)
        a = apos_ref[pl.ds(r, 1), :]
        a2 = jnp.floor(a * (1.0 / 16384.0))
        a1 = jnp.floor((a - 16384.0 * a2) * (1.0 / 128.0))
        a0 = a - 16384.0 * a2 - 128.0 * a1
        g_hi, g_mid, g_lo = _split3(aff_ref[pl.ds(r, 1), :])
        rowv = jnp.zeros((1, LANES), F32) + r.astype(F32)
        zero = jnp.zeros((1, LANES), BF16)
        vals = jnp.concatenate(
            [rowv.astype(BF16), lane.astype(BF16), a2.astype(BF16), a1.astype(BF16), a0.astype(BF16),
             g_hi, g_mid, g_lo] + [zero] * 8, axis=0)
        acc_ref[...] += _dot_nt(vals, onehot)
        return carry

    lax.fori_loop(lo, hi, body, 0)
    acc = acc_ref[...]
    tok = acc[0:1] * 128.0 + acc[1:2] + (g * ng).astype(F32)
    dst = acc[2:3] * 16384.0 + acc[3:4] * 128.0 + acc[4:5] + (g * ec).astype(F32)
    gate = acc[5:6] + acc[6:7] + acc[7:8]
    o_ref[...] = jnp.concatenate(
        [tok.astype(jnp.int32), dst.astype(jnp.int32), pltpu.bitcast(gate, jnp.int32),
         jnp.zeros((SUBLANES - 3, LANES), jnp.int32)], axis=0)


def _compact(bounds, mask, rank, apos, aff_g, cap, ng, n_exp):
    g, rows, _ = mask.shape
    nb = rows // n_exp
    ntile = cap // LANES
    kern = functools.partial(_compact_kernel, n_exp=n_exp, ntile=ntile, ng=ng, ec=n_exp * cap)
    spec = pl.BlockSpec((None, nb, LANES), lambda gi, e, p, b: (gi, e, 0))
    return pl.pallas_call(
        kern,
        out_shape=jax.ShapeDtypeStruct((g, n_exp, ntile, SUBLANES, LANES), jnp.int32),
        grid_spec=pltpu.PrefetchScalarGridSpec(
            num_scalar_prefetch=1,
            grid=(g, n_exp, ntile),
            in_specs=[spec, spec, spec, spec],
            out_specs=pl.BlockSpec((None, None, None, SUBLANES, LANES),
                                   lambda gi, e, p, b: (gi, e, p, 0, 0)),
            scratch_shapes=[pltpu.VMEM((2 * SUBLANES, LANES), F32)],
        ),
        compiler_params=_cparams(("parallel", "parallel", "arbitrary")),
        name="moe_compact",
    )(bounds.reshape(-1), mask, rank, apos, aff_g)


def _ffn_kernel(idx_ref, dst_ref, x_hbm, gn_ref, wg_ref, wu_ref, wd_ref, gate_ref, c_hbm,
                xg_ref, xe_ref, acc_ref, sem_in, sem_out, *, tm, nm):
    e = pl.program_id(0)
    mi = pl.program_id(1)
    f = pl.program_id(2)
    nf = pl.num_programs(2)
    base = (e * nm + mi) * tm

    @pl.when(f == 0)
    def _():
        def issue(i, carry):
            t = idx_ref[base + i]
            pltpu.make_async_copy(x_hbm.at[pl.ds(t, 1)], xg_ref.at[pl.ds(i, 1)], sem_in).start()
            return carry

        lax.fori_loop(0, tm, issue, 0)
        pltpu.make_async_copy(x_hbm.at[pl.ds(0, tm)], xg_ref, sem_in).wait()
        x = xg_ref[...]
        xe_ref[...] = (x * _rms_scale(x) * gn_ref[...]).astype(BF16)
        acc_ref[...] = jnp.zeros_like(acc_ref)

    xe = xe_ref[...]
    hg = _dot(xe, wg_ref[...])
    hu = _dot(xe, wu_ref[...])
    hid = (hg * jax.nn.sigmoid(hg) * hu).astype(BF16)
    acc_ref[...] += _dot(hid, wd_ref[...])

    @pl.when(f == nf - 1)
    def _():
        acc_ref[...] = acc_ref[...] * gate_ref[...]

        def issue(i, carry):
            d = dst_ref[base + i]
            pltpu.make_async_copy(acc_ref.at[pl.ds(i, 1)], c_hbm.at[pl.ds(d, 1)], sem_out).start()
            return carry

        lax.fori_loop(0, tm, issue, 0)
        pltpu.make_async_copy(acc_ref, c_hbm.at[pl.ds(0, tm)], sem_out).wait()


def _expert_ffn(idx, dst, gate, x, gain, wg, wu, wd):
    m, d = x.shape
    n_exp, _, ff = wg.shape
    ct = idx.shape[0] // n_exp
    tm = min(512, ct)
    tf = min(512, ff)
    nm = ct // tm
    kern = functools.partial(_ffn_kernel, tm=tm, nm=nm)
    return pl.pallas_call(
        kern,
        out_shape=jax.ShapeDtypeStruct((n_exp * ct, d), F32),
        grid_spec=pltpu.PrefetchScalarGridSpec(
            num_scalar_prefetch=2,
            grid=(n_exp, nm, ff // tf),
            in_specs=[
                pl.BlockSpec(memory_space=pl.ANY),
                pl.BlockSpec((1, d), lambda e, mi, f, a, b: (0, 0)),
                pl.BlockSpec((None, d, tf), lambda e, mi, f, a, b: (e, 0, f)),
                pl.BlockSpec((None, d, tf), lambda e, mi, f, a, b: (e, 0, f)),
                pl.BlockSpec((None, tf, d), lambda e, mi, f, a, b: (e, f, 0)),
                pl.BlockSpec((tm, 1), lambda e, mi, f, a, b: (e * nm + mi, 0)),
            ],
            out_specs=pl.BlockSpec(memory_space=pl.ANY),
            scratch_shapes=[
                pltpu.VMEM((tm, d), F32),
                pltpu.VMEM((tm, d), BF16),
                pltpu.VMEM((tm, d), F32),
                pltpu.SemaphoreType.DMA(()),
                pltpu.SemaphoreType.DMA(()),
            ],
        ),
        compiler_params=_cparams(("arbitrary", "arbitrary", "arbitrary")),
        name="moe_ffn",
    )(idx, dst, x, gain.reshape(1, d), wg, wu, wd, gate)


def _combine_kernel(lo_ref, x_ref, t0_ref, t1_ref, c_hbm, gn_ref, o_ref, win_ref, acc_ref, sem,
                    *, tb, aw, step, total, final):
    i = pl.program_id(0)
    lo = lo_ref[i]
    hi = lo_ref[i + 1]
    t0 = t0_ref[...]
    t1 = t1_ref[...]
    acc_ref[...] = x_ref[...]
    nchunk = (hi - lo + step - 1) // step
    col = lax.broadcasted_iota(jnp.int32, (1, aw), 1)

    def body(ci, carry):
        own_lo = lo + ci * step
        own_hi = jnp.minimum(own_lo + step, hi)
        w0 = jnp.minimum((own_lo // SUBLANES) * SUBLANES, total - aw)
        w0 = pl.multiple_of(w0, SUBLANES)
        cp = pltpu.make_async_copy(c_hbm.at[pl.ds(w0, aw)], win_ref, sem)
        cp.start()
        cp.wait()
        rowid = col + w0
        sel = ((rowid >= jnp.maximum(t0, own_lo)) & (rowid < jnp.minimum(t1, own_hi)))
        s = jnp.where(sel, 1.0, 0.0).astype(BF16)
        w = win_ref[...]
        w_hi = w.astype(BF16)
        w_lo = (w - w_hi.astype(F32)).astype(BF16)
        acc_ref[...] += _dot(s, w_hi) + _dot(s, w_lo)
        return carry

    lax.fori_loop(0, nchunk, body, 0)
    y = acc_ref[...]
    if final:
        y = y * _rms_scale(y) * gn_ref[...]
    o_ref[...] = y


def _combine(x, contrib, blk_lo, tok0, tok1, final_gain):
    m, d = x.shape
    total = contrib.shape[0]
    tb = min(256, m)
    step = min(512, total - LANES)
    aw = step + LANES
    final = final_gain is not None
    gn = final_gain if final else jnp.ones((d,), F32)
    kern = functools.partial(_combine_kernel, tb=tb, aw=aw, step=step, total=total, final=final)
    return pl.pallas_call(
        kern,
        out_shape=jax.ShapeDtypeStruct((m, d), F32),
        grid_spec=pltpu.PrefetchScalarGridSpec(
            num_scalar_prefetch=1,
            grid=(m // tb,),
            in_specs=[
                pl.BlockSpec((tb, d), lambda i, b: (i, 0)),
                pl.BlockSpec((tb, 1), lambda i, b: (i, 0)),
                pl.BlockSpec((tb, 1), lambda i, b: (i, 0)),
                pl.BlockSpec(memory_space=pl.ANY),
                pl.BlockSpec((1, d), lambda i, b: (0, 0)),
            ],
            out_specs=pl.BlockSpec((tb, d), lambda i, b: (i, 0)),
            scratch_shapes=[
                pltpu.VMEM((aw, d), F32),
                pltpu.VMEM((tb, d), F32),
                pltpu.SemaphoreType.DMA(()),
            ],
        ),
        compiler_params=_cparams(("arbitrary",)),
        name="moe_combine",
    )(blk_lo, x, tok0, tok1, contrib, gn.reshape(1, d))


def _ec_moe(x, gain, rw, wg, wu, wd, group_sizes, final_gain):
    m, d = x.shape
    n_exp = rw.shape[1]
    ng = group_sizes[0]
    assert all(s == ng for s in group_sizes)
    ngroups = len(group_sizes)
    nb = ng // LANES
    cap = max(1, CAPACITY_FACTOR * ng // n_exp)
    aff = _router(x, gain, rw.T)
    aff_g = aff.reshape(n_exp, ngroups, nb, LANES).transpose(1, 0, 2, 3).reshape(ngroups, n_exp * nb, LANES)
    mask, rank, apos, tokoff, ktot, bounds = _select(aff_g, cap, n_exp)
    packed = _compact(bounds, mask, rank, apos, aff_g, cap, ng, n_exp)
    lists = packed.transpose(3, 1, 0, 2, 4).reshape(SUBLANES, n_exp * ngroups * cap)
    idx = lists[0]
    dst = lists[1]
    gate = lax.bitcast_convert_type(lists[2], F32).reshape(-1, 1)
    contrib = _expert_ffn(idx, dst, gate, x, gain, wg, wu, wd)
    ec = n_exp * cap
    goff = (jnp.arange(ngroups, dtype=F32) * ec).reshape(ngroups, 1, 1)
    t0 = (tokoff + goff).reshape(m).astype(jnp.int32)
    t1 = (tokoff + ktot + goff).reshape(m).astype(jnp.int32)
    tb = min(256, m)
    blk_lo = jnp.concatenate([t0[::tb], jnp.full((1,), ngroups * ec, jnp.int32)])
    return _combine(x, contrib, blk_lo, t0.reshape(m, 1), t1.reshape(m, 1), final_gain)


def kernel(x_prompt, x_sample, norm_mix, norm_ffn, norm_final, hgrn_w_in, hgrn_lb, hgrn_onorm,
           hgrn_w_out, conv_w_in, conv_w, conv_w_out, router_w, w_gate, w_up, w_down):
    d = x_prompt.shape[-1]
    groups = [x_prompt, x_sample]
    group_sizes = [g.shape[0] * g.shape[1] for g in groups]
    starts, ends, off = [], [], 0
    for g in groups:
        for b in range(g.shape[0]):
            starts.append(off + b * g.shape[1])
            ends.append(off + (b + 1) * g.shape[1])
        off += g.shape[0] * g.shape[1]
    starts, ends = tuple(starts), tuple(ends)
    x = jnp.concatenate([g.reshape(-1, d) for g in groups], axis=0)
    depth = norm_mix.shape[0]
    assert d == HGRN_HEADS * HEAD_DIM and hgrn_lb.shape[-1] == d

    lbs = _lower_bounds(hgrn_lb)
    for layer in range(depth):
        j = layer // 2
        if layer % 2 == 0:
            proj = _norm_matmul(x, norm_mix[layer], hgrn_w_in[j].astype(BF16))
            o_fwd = _hgrn_pass(proj, lbs[j, 0], None, None, starts, ends, reverse=False)
            gated = _hgrn_pass(proj, lbs[j, 1], o_fwd, hgrn_onorm[j], starts, ends, reverse=True)
            x = _matmul_residual(gated, hgrn_w_out[j].astype(BF16), x)
        else:
            proj = _norm_matmul(x, norm_mix[layer], conv_w_in[j].astype(BF16))
            x = _conv_out(proj, conv_w[j], conv_w_out[j].astype(BF16), x, starts, ends)
        x = _ec_moe(x, norm_ffn[layer], router_w[layer], w_gate[layer].astype(BF16),
                    w_up[layer].astype(BF16), w_down[layer].astype(BF16), group_sizes,
                    norm_final if layer == depth - 1 else None)
    outs, off = [], 0
    for g in groups:
        n = g.shape[0] * g.shape[1]
        outs.append(x[off:off + n].reshape(g.shape))
        off += n
    return tuple(outs)
```

```python
import functools

import jax
import jax.numpy as jnp
from jax import lax
from jax.experimental import pallas as pl
from jax.experimental.pallas import tpu as pltpu

HGRN_HEADS = 16
HEAD_DIM = 128
CHUNK = 64
CAPACITY_FACTOR = 2
EPS = 1e-6

LANES = 128
SUBLANES = 8
VMEM_LIMIT = 56 * 1024 * 1024

F32 = jnp.float32
BF16 = jnp.bfloat16


def _cparams(sem):
    return pltpu.CompilerParams(dimension_semantics=sem, vmem_limit_bytes=VMEM_LIMIT)


def _rms_scale(x):
    return lax.rsqrt(jnp.mean(x * x, axis=-1, keepdims=True) + EPS)


def _split3(x):
    hi = x.astype(BF16)
    r1 = x - hi.astype(F32)
    mid = r1.astype(BF16)
    lo = (r1 - mid.astype(F32)).astype(BF16)
    return hi, mid, lo


def _dot(a, b):
    return jnp.dot(a, b, preferred_element_type=F32)


def _dot_nt(a, b):
    return lax.dot_general(a, b, (((1,), (1,)), ((), ())), preferred_element_type=F32)


def _dot_tn(a, b):
    return lax.dot_general(a, b, (((0,), (0,)), ((), ())), preferred_element_type=F32)


def _lbs_kernel(lb_ref, o_ref):
    n = lb_ref.shape[0]
    rows = [lb_ref[j] for j in range(n)]
    m = functools.reduce(jnp.maximum, rows)
    ex = [jnp.exp(r - m) for r in rows]
    tot = functools.reduce(lambda a, b: a + b, ex)
    sm = [e / tot for e in ex]
    acc = sm[0]
    o_ref[0] = acc - sm[0]
    for j in range(1, n):
        acc = acc + sm[j]
        o_ref[j] = acc - sm[0]


def _lower_bounds(hgrn_lb):
    return pl.pallas_call(
        _lbs_kernel, out_shape=jax.ShapeDtypeStruct(hgrn_lb.shape, F32), name="hgrn_lbs",
    )(hgrn_lb.astype(F32))


def _norm_matmul_kernel(x_ref, g_ref, w_ref, o_ref, h_ref):
    @pl.when(pl.program_id(1) == 0)
    def _():
        x = x_ref[...]
        h_ref[...] = (x * _rms_scale(x) * g_ref[...]).astype(BF16)

    o_ref[...] = _dot(h_ref[...], w_ref[...])


def _tile(n, pref, unit=LANES):
    t = min(pref, n) // unit * unit
    while n % t:
        t -= unit
    return t


def _norm_matmul(x, gain, w):
    m, k = x.shape
    n = w.shape[1]
    tm = _tile(m, 1024, SUBLANES)
    tn = _tile(n, 1024)
    return pl.pallas_call(
        _norm_matmul_kernel,
        out_shape=jax.ShapeDtypeStruct((m, n), F32),
        grid=(m // tm, n // tn),
        in_specs=[
            pl.BlockSpec((tm, k), lambda i, j: (i, 0)),
            pl.BlockSpec((1, k), lambda i, j: (0, 0)),
            pl.BlockSpec((k, tn), lambda i, j: (0, j)),
        ],
        out_specs=pl.BlockSpec((tm, tn), lambda i, j: (i, j)),
        scratch_shapes=[pltpu.VMEM((tm, k), BF16)],
        compiler_params=_cparams(("parallel", "arbitrary")),
        name="norm_matmul",
    )(x, gain.reshape(1, k), w)


def _matmul_res_kernel(a_ref, w_ref, r_ref, o_ref):
    o_ref[...] = r_ref[...] + _dot(a_ref[...], w_ref[...])


def _matmul_residual(a, w, res):
    m, k = a.shape
    n = w.shape[1]
    tm = min(512, m)
    return pl.pallas_call(
        _matmul_res_kernel,
        out_shape=jax.ShapeDtypeStruct((m, n), F32),
        grid=(m // tm,),
        in_specs=[
            pl.BlockSpec((tm, k), lambda i: (i, 0)),
            pl.BlockSpec((k, n), lambda i: (0, 0)),
            pl.BlockSpec((tm, n), lambda i: (i, 0)),
        ],
        out_specs=pl.BlockSpec((tm, n), lambda i: (i, 0)),
        compiler_params=_cparams(("parallel",)),
        name="matmul_residual",
    )(a, w, res)


def _conv_out_kernel(b_ref, c_ref, u_ref, cp_ref, up_ref, cn_ref, un_ref, cw_ref, w_ref, x_ref,
                     o_ref, *, tm, starts, ends):
    tok0 = pl.program_id(0) * tm
    is_start = functools.reduce(jnp.logical_or, [tok0 == s for s in starts])
    is_end = functools.reduce(jnp.logical_or, [tok0 + tm == e for e in ends])
    z = c_ref[...] * u_ref[...]
    z_before = cp_ref[SUBLANES - 1:SUBLANES, :] * up_ref[SUBLANES - 1:SUBLANES, :]
    z_after = cn_ref[0:1, :] * un_ref[0:1, :]
    z_before = jnp.where(is_start, 0.0, z_before)
    z_after = jnp.where(is_end, 0.0, z_after)
    row = lax.broadcasted_iota(jnp.int32, z.shape, 0)
    z_prev = jnp.where(row == 0, z_before, pltpu.roll(z, 1, axis=0))
    z_next = jnp.where(row == tm - 1, z_after, pltpu.roll(z, tm - 1, axis=0))
    zc = z_prev * cw_ref[0:1, :] + z * cw_ref[1:2, :] + z_next * cw_ref[2:3, :]
    a = (b_ref[...] * zc).astype(BF16)
    o_ref[...] = x_ref[...] + _dot(a, w_ref[...])


def _conv_out(proj, conv_w, w_out, x, starts, ends):
    m, d = x.shape
    tm = min(256, m)
    nsub = tm // SUBLANES
    last_sub = m // SUBLANES - 1
    kern = functools.partial(_conv_out_kernel, tm=tm, starts=starts, ends=ends)
    prev_map = lambda sec: (lambda i: (jnp.maximum(i * nsub - 1, 0), sec))
    next_map = lambda sec: (lambda i: (jnp.minimum((i + 1) * nsub, last_sub), sec))
    return pl.pallas_call(
        kern,
        out_shape=jax.ShapeDtypeStruct((m, d), F32),
        grid=(m // tm,),
        in_specs=[
            pl.BlockSpec((tm, d), lambda i: (i, 0)),
            pl.BlockSpec((tm, d), lambda i: (i, 1)),
            pl.BlockSpec((tm, d), lambda i: (i, 2)),
            pl.BlockSpec((SUBLANES, d), prev_map(1)),
            pl.BlockSpec((SUBLANES, d), prev_map(2)),
            pl.BlockSpec((SUBLANES, d), next_map(1)),
            pl.BlockSpec((SUBLANES, d), next_map(2)),
            pl.BlockSpec((3, d), lambda i: (0, 0)),
            pl.BlockSpec((d, d), lambda i: (0, 0)),
            pl.BlockSpec((tm, d), lambda i: (i, 0)),
        ],
        out_specs=pl.BlockSpec((tm, d), lambda i: (i, 0)),
        compiler_params=_cparams(("parallel",)),
        name="conv_out",
    )(proj, proj, proj, proj, proj, proj, proj, conv_w, w_out, x)


def _hgrn_kernel(*refs, reverse, tt, hb, bounds, final):
    if final:
        q_ref, f_ref, v_ref, lb_ref, tri_ref, of_ref, g_ref, on_ref, o_ref, s_ref = refs
    else:
        q_ref, f_ref, v_ref, lb_ref, tri_ref, o_ref, s_ref = refs
    i = pl.program_id(1)
    blk = (pl.num_programs(1) - 1 - i) if reverse else i
    edge = (blk + 1) * tt if reverse else blk * tt
    reset = functools.reduce(jnp.logical_or, [edge == b for b in bounds])

    @pl.when(reset)
    def _():
        s_ref[...] = jnp.zeros_like(s_ref)

    lb = lb_ref[...]
    fp = f_ref[...]
    sig = jax.nn.sigmoid(fp)
    logf = jnp.log(lb + (1.0 - lb) * sig)
    kk = (1.0 - lb) * (1.0 - sig)
    qr = q_ref[...]
    q = qr * jax.nn.sigmoid(qr)

    tri = tri_ref[...]
    l_hi = logf.astype(BF16)
    l_lo = (logf - l_hi.astype(F32)).astype(BF16)
    cum = _dot(tri, l_hi) + _dot(tri, l_lo)

    rc = lax.broadcasted_iota(jnp.int32, (CHUNK, CHUNK), 0)
    cc = lax.broadcasted_iota(jnp.int32, (CHUNK, CHUNK), 1)
    causal = (cc >= rc) if reverse else (cc <= rc)
    ref_row = CHUNK // 2 if reverse else CHUNK // 2 - 1
    last_row = 0 if reverse else CHUNK - 1

    nchunk = tt // CHUNK
    order = range(nchunk - 1, -1, -1) if reverse else range(nchunk)
    for h in range(hb):
        ls = slice(h * HEAD_DIM, (h + 1) * HEAD_DIM)
        st = s_ref[h]
        for ci in order:
            ts = slice(ci * CHUNK, (ci + 1) * CHUNK)
            cum_c = cum[ts, ls]
            q_c = q[ts, ls]
            k_c = kk[ts, ls]
            v_c = v_ref[ts, ls].astype(BF16)
            ref = cum_c[ref_row:ref_row + 1, :]
            cl = cum_c[last_row:last_row + 1, :]
            qa = q_c * jnp.exp(cum_c - ref)
            kb = k_c * jnp.exp(ref - cum_c)
            scores = jnp.where(causal, _dot_nt(qa.astype(BF16), kb.astype(BF16)), 0.0)
            o = _dot(scores.astype(BF16), v_c)
            qd = (qa * jnp.exp(ref)).astype(BF16)
            o = o + _dot_nt(qd, st.astype(BF16))
            kd = (kb * jnp.exp(cl - ref)).astype(BF16)
            st = jnp.exp(cl) * st + _dot_tn(v_c, kd)
            if final:
                ot = of_ref[ts, ls] + o
                gr = g_ref[ts, ls]
                y = ot * _rms_scale(ot) * on_ref[:, ls] * (gr * jax.nn.sigmoid(gr))
                o_ref[ts, ls] = y.astype(o_ref.dtype)
            else:
                o_ref[ts, ls] = o
        s_ref[h] = st


def _hgrn_pass(proj, lb, o_fwd, onorm, starts, ends, reverse):
    m = proj.shape[0]
    d = proj.shape[1] // 5
    hb = min(4, HGRN_HEADS)
    wl = hb * HEAD_DIM
    tt = 256
    nblk = m // tt
    nh = d // wl
    final = reverse
    bmap = (lambda i: nblk - 1 - i) if reverse else (lambda i: i)
    col = lambda sec: (lambda h, i: (bmap(i), sec * nh + h))
    fsec = 2 if reverse else 1
    in_specs = [
        pl.BlockSpec((tt, wl), col(0)),
        pl.BlockSpec((tt, wl), col(fsec)),
        pl.BlockSpec((tt, wl), col(3)),
        pl.BlockSpec((1, wl), lambda h, i: (0, h)),
        pl.BlockSpec((tt, tt), lambda h, i: (0, 0)),
    ]
    r = lax.broadcasted_iota(jnp.int32, (tt, tt), 0)
    c = lax.broadcasted_iota(jnp.int32, (tt, tt), 1)
    ordered = (c >= r) if reverse else (c <= r)
    tri = jnp.where(((r // CHUNK) == (c // CHUNK)) & ordered, 1.0, 0.0).astype(BF16)
    args = [proj, proj, proj, lb.reshape(1, d), tri]
    if final:
        in_specs += [
            pl.BlockSpec((tt, wl), lambda h, i: (bmap(i), h)),
            pl.BlockSpec((tt, wl), col(4)),
            pl.BlockSpec((1, wl), lambda h, i: (0, h)),
        ]
        args += [o_fwd, proj, onorm.reshape(1, d)]
    kern = functools.partial(_hgrn_kernel, reverse=reverse, tt=tt, hb=hb,
                             bounds=ends if reverse else starts, final=final)
    return pl.pallas_call(
        kern,
        out_shape=jax.ShapeDtypeStruct((m, d), BF16 if final else F32),
        grid=(nh, nblk),
        in_specs=in_specs,
        out_specs=pl.BlockSpec((tt, wl), lambda h, i: (bmap(i), h)),
        scratch_shapes=[pltpu.VMEM((hb, HEAD_DIM, HEAD_DIM), F32)],
        compiler_params=_cparams(("parallel", "arbitrary")),
        name="hgrn_bwd" if reverse else "hgrn_fwd",
    )(*args)


def _router_kernel(x_ref, g_ref, rw_ref, aff_ref):
    x = x_ref[...]
    h = x * _rms_scale(x) * g_ref[...]
    logits = lax.dot_general(rw_ref[...], h, (((1,), (1,)), ((), ())),
                             precision=lax.Precision.HIGHEST, preferred_element_type=F32)
    mx = jnp.max(logits, axis=0, keepdims=True)
    p = jnp.exp(logits - mx)
    aff_ref[...] = p / jnp.sum(p, axis=0, keepdims=True)


def _router(x, gain, rw_t):
    m, d = x.shape
    e = rw_t.shape[0]
    tm = min(512, m)
    return pl.pallas_call(
        _router_kernel,
        out_shape=jax.ShapeDtypeStruct((e, m), F32),
        grid=(m // tm,),
        in_specs=[
            pl.BlockSpec((tm, d), lambda i: (i, 0)),
            pl.BlockSpec((1, d), lambda i: (0, 0)),
            pl.BlockSpec((e, d), lambda i: (0, 0)),
        ],
        out_specs=pl.BlockSpec((e, tm), lambda i: (0, i)),
        compiler_params=_cparams(("parallel",)),
        name="moe_router",
    )(x, gain.reshape(1, d), rw_t)


def _select_kernel(aff_ref, mask_ref, rank_ref, apos_ref, tokoff_ref, ktot_ref, tile_ref,
                   *, n_exp, nb, cap):
    rows = n_exp * nb
    aff = aff_ref[...]
    b3 = pltpu.bitcast(aff, jnp.int32).reshape(n_exp, nb, LANES)

    def count3(m3):
        s = jnp.sum(jnp.where(m3, 1.0, 0.0), axis=1, keepdims=True)
        return jnp.sum(s, axis=2, keepdims=True)

    def body(i, t):
        cand = t | jnp.left_shift(jnp.int32(1), 30 - i)
        return jnp.where(count3(b3 >= cand) >= cap, cand, t)

    thr = lax.fori_loop(0, 31, body, jnp.zeros((n_exp, 1, 1), jnp.int32))
    gt3 = b3 > thr
    eq3 = b3 == thr
    need = cap - count3(gt3)

    lane_r = lax.broadcasted_iota(jnp.int32, (LANES, LANES), 0)
    lane_c = lax.broadcasted_iota(jnp.int32, (LANES, LANES), 1)
    upper = jnp.where(lane_r < lane_c, 1.0, 0.0).astype(BF16)
    ones = jnp.ones((LANES, LANES), BF16)
    br = lax.broadcasted_iota(jnp.int32, (nb, nb), 0)
    bc = lax.broadcasted_iota(jnp.int32, (nb, nb), 1)
    lower_nb = jnp.where(bc < br, 1.0, 0.0).astype(BF16)

    def excl_cumsum(m2):
        mb = m2.astype(BF16)
        within = _dot(mb, upper)
        rowtot = _dot(mb, ones)
        rt = rowtot.astype(BF16)
        rowoff = jnp.concatenate(
            [_dot(lower_nb, rt[e * nb:(e + 1) * nb]) for e in range(n_exp)], axis=0)
        return within + rowoff, rowoff, rowtot

    eq2 = jnp.where(eq3, 1.0, 0.0).reshape(rows, LANES)
    eq_rank, _, _ = excl_cumsum(eq2)
    take_eq = eq3 & (eq_rank.reshape(n_exp, nb, LANES) < need)
    m3 = jnp.where(gt3 | take_eq, 1.0, 0.0)
    m2 = m3.reshape(rows, LANES)
    rank, rowoff, rowtot = excl_cumsum(m2)
    mask_ref[...] = m2
    rank_ref[...] = rank

    kbefore = []
    run = jnp.zeros((nb, LANES), F32)
    for e in range(n_exp):
        kbefore.append(run)
        run = run + m3[e]
    ktot = run
    ktot_ref[...] = ktot
    kb = ktot.astype(BF16)
    t_within = _dot(kb, upper)
    t_rowtot = _dot(kb, ones)
    t_hi = jnp.floor(t_rowtot * (1.0 / 64.0))
    t_lo = t_rowtot - 64.0 * t_hi
    t_rowoff = 64.0 * _dot(lower_nb, t_hi.astype(BF16)) + _dot(lower_nb, t_lo.astype(BF16))
    tokoff = t_within + t_rowoff
    tokoff_ref[...] = tokoff
    for e in range(n_exp):
        apos_ref[e * nb:(e + 1) * nb, :] = tokoff + kbefore[e]

    ntile = cap // LANES
    tile_ref[...] = jnp.minimum(jnp.floor(rowoff * (1.0 / LANES)), ntile - 1.0).astype(jnp.int32)


def _select(aff_g, cap, n_exp):
    g, rows, _ = aff_g.shape
    nb = rows // n_exp
    assert cap % LANES == 0 and nb % SUBLANES == 0
    kern = functools.partial(_select_kernel, n_exp=n_exp, nb=nb, cap=cap)
    big = pl.BlockSpec((None, rows, LANES), lambda i: (i, 0, 0))
    small = pl.BlockSpec((None, nb, LANES), lambda i: (i, 0, 0))
    return pl.pallas_call(
        kern,
        out_shape=(
            jax.ShapeDtypeStruct((g, rows, LANES), F32),
            jax.ShapeDtypeStruct((g, rows, LANES), F32),
            jax.ShapeDtypeStruct((g, rows, LANES), F32),
            jax.ShapeDtypeStruct((g, nb, LANES), F32),
            jax.ShapeDtypeStruct((g, nb, LANES), F32),
            jax.ShapeDtypeStruct((g, rows, LANES), jnp.int32),
        ),
        grid=(g,),
        in_specs=[big],
        out_specs=(big, big, big, small, small, big),
        compiler_params=_cparams(("parallel",)),
        name="moe_select",
    )(aff_g)


def _compact_kernel(tile_ref, mask_ref, rank_ref, apos_ref, aff_ref, o_ref, acc_ref,
                    *, n_exp, nb, ntile, ng, ec, unroll):
    g = pl.program_id(0)
    e = pl.program_id(1)
    base = (g * n_exp + e) * nb
    slot = lax.broadcasted_iota(jnp.int32, (2 * LANES, LANES), 0)
    lane = lax.broadcasted_iota(jnp.int32, (1, LANES), 1).astype(F32)
    acc_ref[...] = jnp.zeros_like(acc_ref)

    def row(r):
        p0 = tile_ref[base + r]
        rank_r = rank_ref[pl.ds(r, 1), :].astype(jnp.int32) - p0 * LANES
        mask_r = mask_ref[pl.ds(r, 1), :]
        onehot = jnp.where((rank_r == slot) & (mask_r > 0.5), 1.0, 0.0).astype(BF16)
        a = apos_ref[pl.ds(r, 1), :]
        a2 = jnp.floor(a * (1.0 / 16384.0))
        a1 = jnp.floor((a - 16384.0 * a2) * (1.0 / 128.0))
        a0 = a - 16384.0 * a2 - 128.0 * a1
        g_hi, g_mid, g_lo = _split3(aff_ref[pl.ds(r, 1), :])
        rowv = jnp.zeros((1, LANES), F32) + jnp.asarray(r, F32)
        zero = jnp.zeros((1, LANES), BF16)
        vals = jnp.concatenate(
            [rowv.astype(BF16), lane.astype(BF16), a2.astype(BF16), a1.astype(BF16), a0.astype(BF16),
             g_hi, g_mid, g_lo] + [zero] * 8, axis=0)
        res = _dot_nt(vals, onehot)
        acc_ref[p0] += res[:, :LANES]
        acc_ref[p0 + 1] += res[:, LANES:]

    def body(j, carry):
        for u in range(unroll):
            row(j * unroll + u)
        return carry

    lax.fori_loop(0, nb // unroll, body, 0)
    for p in range(ntile):
        acc = acc_ref[p]
        tok = acc[0:1] * 128.0 + acc[1:2] + jnp.asarray(g * ng, F32)
        dst = acc[2:3] * 16384.0 + acc[3:4] * 128.0 + acc[4:5] + jnp.asarray(g * ec, F32)
        gate = acc[5:6] + acc[6:7] + acc[7:8]
        o_ref[p] = jnp.concatenate(
            [tok.astype(jnp.int32), dst.astype(jnp.int32), pltpu.bitcast(gate, jnp.int32),
             jnp.zeros((SUBLANES - 3, LANES), jnp.int32)], axis=0)


def _compact(row_tile, mask, rank, apos, aff_g, cap, ng, n_exp):
    g, rows, _ = mask.shape
    nb = rows // n_exp
    ntile = cap // LANES
    unroll = SUBLANES
    kern = functools.partial(_compact_kernel, n_exp=n_exp, nb=nb, ntile=ntile, ng=ng,
                             ec=n_exp * cap, unroll=unroll)
    spec = pl.BlockSpec((None, nb, LANES), lambda gi, e, b: (gi, e, 0))
    return pl.pallas_call(
        kern,
        out_shape=jax.ShapeDtypeStruct((g, n_exp, ntile, SUBLANES, LANES), jnp.int32),
        grid_spec=pltpu.PrefetchScalarGridSpec(
            num_scalar_prefetch=1,
            grid=(g, n_exp),
            in_specs=[spec, spec, spec, spec],
            out_specs=pl.BlockSpec((None, None, ntile, SUBLANES, LANES),
                                   lambda gi, e, b: (gi, e, 0, 0, 0)),
            scratch_shapes=[pltpu.VMEM((ntile + 1, 2 * SUBLANES, LANES), F32)],
        ),
        compiler_params=_cparams(("parallel", "arbitrary")),
        name="moe_compact",
    )(row_tile.reshape(-1), mask, rank, apos, aff_g)


def _ffn_kernel(idx_ref, dst_ref, x_hbm, gn_ref, wg_ref, wu_ref, wd_ref, gate_ref, c_hbm,
                xg_ref, xe_ref, acc_ref, yb_ref, sem_in, sem_out, *, tm, nt, rps, trash):
    n = pl.program_id(0)
    f = pl.program_id(1)
    nf = pl.num_programs(1)
    slot = n % 2
    other = 1 - slot

    def gather_row(tile, i, buf):
        t = idx_ref[tile * tm + i]
        pltpu.make_async_copy(x_hbm.at[pl.ds(t, 1)], xg_ref.at[buf, pl.ds(i, 1)], sem_in.at[buf]).start()

    def scatter_row(tile, i, buf, valid):
        d = jnp.where(valid, dst_ref[tile * tm + i], trash + i)
        pltpu.make_async_copy(yb_ref.at[buf, pl.ds(i, 1)], c_hbm.at[pl.ds(d, 1)], sem_out.at[buf]).start()

    def wait_gather(buf):
        pltpu.make_async_copy(x_hbm.at[pl.ds(0, tm)], xg_ref.at[buf], sem_in.at[buf]).wait()

    def wait_scatter(buf):
        pltpu.make_async_copy(yb_ref.at[buf], c_hbm.at[pl.ds(0, tm)], sem_out.at[buf]).wait()

    @pl.when((n == 0) & (f == 0))
    def _():
        yb_ref[1] = jnp.zeros(yb_ref.shape[1:], F32)

        def issue(i, carry):
            gather_row(0, i, 0)
            return carry

        lax.fori_loop(0, tm, issue, 0)

    @pl.when(n < nt)
    def _():
        @pl.when(f == 0)
        def _():
            wait_gather(slot)
            x = xg_ref[slot]
            xe_ref[...] = (x * _rms_scale(x) * gn_ref[...]).astype(BF16)
            acc_ref[...] = jnp.zeros_like(acc_ref)

        nxt = jnp.minimum(n + 1, nt - 1)
        prv = jnp.maximum(n - 1, 0)
        for r in range(rps):
            i = f * rps + r
            gather_row(nxt, i, other)
            scatter_row(prv, i, other, n > 0)
        xe = xe_ref[...]
        hg = _dot(xe, wg_ref[...])
        hu = _dot(xe, wu_ref[...])
        hid = (hg * jax.nn.sigmoid(hg) * hu).astype(BF16)
        acc_ref[...] += _dot(hid, wd_ref[...])

        @pl.when(f == nf - 1)
        def _():
            @pl.when(n >= 1)
            def _():
                wait_scatter(slot)

            yb_ref[slot] = acc_ref[...] * gate_ref[...]

    @pl.when((n == nt) & (f == 0))
    def _():
        wait_gather(slot)

        def issue(i, carry):
            scatter_row(nt - 1, i, other, True)
            return carry

        lax.fori_loop(0, tm, issue, 0)
        wait_scatter(slot)
        wait_scatter(other)


def _expert_ffn(idx, dst, gate, x, gain, wg, wu, wd):
    m, d = x.shape
    n_exp, _, ff = wg.shape
    ct = idx.shape[0] // n_exp
    tm = min(512, ct)
    tf = min(512, ff)
    nm = ct // tm
    nt = n_exp * nm
    nf = ff // tf
    assert nt >= 2 and tm % nf == 0
    kern = functools.partial(_ffn_kernel, tm=tm, nt=nt, rps=tm // nf, trash=n_exp * ct)
    tile = lambda n: jnp.minimum(n, nt - 1)
    return pl.pallas_call(
        kern,
        out_shape=jax.ShapeDtypeStruct((n_exp * ct + tm, d), F32),
        grid_spec=pltpu.PrefetchScalarGridSpec(
            num_scalar_prefetch=2,
            grid=(nt + 1, nf),
            in_specs=[
                pl.BlockSpec(memory_space=pl.ANY),
                pl.BlockSpec((1, d), lambda n, f, a, b: (0, 0)),
                pl.BlockSpec((None, d, tf), lambda n, f, a, b: (tile(n) // nm, 0, f)),
                pl.BlockSpec((None, d, tf), lambda n, f, a, b: (tile(n) // nm, 0, f)),
                pl.BlockSpec((None, tf, d), lambda n, f, a, b: (tile(n) // nm, f, 0)),
                pl.BlockSpec((tm, 1), lambda n, f, a, b: (tile(n), 0)),
            ],
            out_specs=pl.BlockSpec(memory_space=pl.ANY),
            scratch_shapes=[
                pltpu.VMEM((2, tm, d), F32),
                pltpu.VMEM((tm, d), BF16),
                pltpu.VMEM((tm, d), F32),
                pltpu.VMEM((2, tm, d), F32),
                pltpu.SemaphoreType.DMA((2,)),
                pltpu.SemaphoreType.DMA((2,)),
            ],
        ),
        compiler_params=_cparams(("arbitrary", "arbitrary")),
        name="moe_ffn",
    )(idx, dst, x, gain.reshape(1, d), wg, wu, wd, gate)


def _combine_kernel(lo_ref, x_ref, t0_ref, t1_ref, c_hbm, gn_ref, o_ref, win_ref, acc_ref, sem,
                    *, cw, limit, final):
    i = pl.program_id(0)
    nblk = pl.num_programs(0)

    def window(blk, c, buf):
        w = (lo_ref[blk] // SUBLANES) * SUBLANES + c * cw
        w = pl.multiple_of(jnp.minimum(w, limit - cw), SUBLANES)
        return pltpu.make_async_copy(c_hbm.at[pl.ds(w, cw)], win_ref.at[buf], sem.at[buf])

    @pl.when(i == 0)
    def _():
        window(0, 0, 0).start()

    lo = lo_ref[i]
    hi = lo_ref[i + 1]
    base = (lo // SUBLANES) * SUBLANES
    nchunk = jnp.maximum((hi - base + cw - 1) // cw, 1)
    t0 = t0_ref[...]
    t1 = t1_ref[...]
    acc_ref[...] = x_ref[...]
    col = lax.broadcasted_iota(jnp.int32, (1, cw), 1)

    def body(c, carry):
        buf = c % 2
        window(i, c, buf).wait()

        @pl.when(c + 1 < nchunk)
        def _():
            window(i, c + 1, 1 - buf).start()

        own_lo = base + c * cw
        own_hi = jnp.minimum(own_lo + cw, hi)
        rowid = col + jnp.minimum(own_lo, limit - cw)
        sel = (rowid >= jnp.maximum(t0, own_lo)) & (rowid < jnp.minimum(t1, own_hi))
        s = jnp.where(sel, 1.0, 0.0).astype(BF16)
        w = win_ref[buf]
        w_hi = w.astype(BF16)
        w_lo = (w - w_hi.astype(F32)).astype(BF16)
        acc_ref[...] += _dot(s, w_hi) + _dot(s, w_lo)
        return carry

    lax.fori_loop(0, nchunk, body, 0)

    @pl.when(i + 1 < nblk)
    def _():
        window(i + 1, 0, 0).start()

    y = acc_ref[...]
    if final:
        y = y * _rms_scale(y) * gn_ref[...]
    o_ref[...] = y


def _combine(x, contrib, blk_lo, tok0, tok1, final_gain):
    m, d = x.shape
    limit = contrib.shape[0]
    tb = min(256, m)
    cw = 256
    final = final_gain is not None
    gn = final_gain if final else jnp.ones((d,), F32)
    kern = functools.partial(_combine_kernel, cw=cw, limit=limit, final=final)
    return pl.pallas_call(
        kern,
        out_shape=jax.ShapeDtypeStruct((m, d), F32),
        grid_spec=pltpu.PrefetchScalarGridSpec(
            num_scalar_prefetch=1,
            grid=(m // tb,),
            in_specs=[
                pl.BlockSpec((tb, d), lambda i, b: (i, 0)),
                pl.BlockSpec((tb, 1), lambda i, b: (i, 0)),
                pl.BlockSpec((tb, 1), lambda i, b: (i, 0)),
                pl.BlockSpec(memory_space=pl.ANY),
                pl.BlockSpec((1, d), lambda i, b: (0, 0)),
            ],
            out_specs=pl.BlockSpec((tb, d), lambda i, b: (i, 0)),
            scratch_shapes=[
                pltpu.VMEM((2, cw, d), F32),
                pltpu.VMEM((tb, d), F32),
                pltpu.SemaphoreType.DMA((2,)),
            ],
        ),
        compiler_params=_cparams(("arbitrary",)),
        name="moe_combine",
    )(blk_lo, x, tok0, tok1, contrib, gn.reshape(1, d))


def _ec_moe(x, gain, rw, wg, wu, wd, group_sizes, final_gain):
    m, d = x.shape
    n_exp = rw.shape[1]
    ng = group_sizes[0]
    assert all(s == ng for s in group_sizes)
    ngroups = len(group_sizes)
    nb = ng // LANES
    cap = max(1, CAPACITY_FACTOR * ng // n_exp)
    aff = _router(x, gain, rw.T)
    aff_g = aff.reshape(n_exp, ngroups, nb, LANES).transpose(1, 0, 2, 3).reshape(ngroups, n_exp * nb, LANES)
    mask, rank, apos, tokoff, ktot, row_tile = _select(aff_g, cap, n_exp)
    packed = _compact(row_tile[:, :, 0], mask, rank, apos, aff_g, cap, ng, n_exp)
    lists = packed.transpose(3, 1, 0, 2, 4).reshape(SUBLANES, n_exp * ngroups * cap)
    idx = lists[0]
    dst = lists[1]
    gate = lax.bitcast_convert_type(lists[2], F32).reshape(-1, 1)
    contrib = _expert_ffn(idx, dst, gate, x, gain, wg, wu, wd)
    ec = n_exp * cap
    goff = (jnp.arange(ngroups, dtype=F32) * ec).reshape(ngroups, 1, 1)
    t0 = (tokoff + goff).reshape(m).astype(jnp.int32)
    t1 = (tokoff + ktot + goff).reshape(m).astype(jnp.int32)
    tb = min(256, m)
    blk_lo = jnp.concatenate([t0[::tb], jnp.full((1,), ngroups * ec, jnp.int32)])
    return _combine(x, contrib, blk_lo, t0.reshape(m, 1), t1.reshape(m, 1), final_gain)


def kernel(x_prompt, x_sample, norm_mix, norm_ffn, norm_final, hgrn_w_in, hgrn_lb, hgrn_onorm,
           hgrn_w_out, conv_w_in, conv_w, conv_w_out, router_w, w_gate, w_up, w_down):
    d = x_prompt.shape[-1]
    groups = [x_prompt, x_sample]
    group_sizes = [g.shape[0] * g.shape[1] for g in groups]
    starts, ends, off = [], [], 0
    for g in groups:
        for b in range(g.shape[0]):
            starts.append(off + b * g.shape[1])
            ends.append(off + (b + 1) * g.shape[1])
        off += g.shape[0] * g.shape[1]
    starts, ends = tuple(starts), tuple(ends)
    x = jnp.concatenate([g.reshape(-1, d) for g in groups], axis=0)
    depth = norm_mix.shape[0]
    assert d == HGRN_HEADS * HEAD_DIM and hgrn_lb.shape[-1] == d

    lbs = _lower_bounds(hgrn_lb)
    for layer in range(depth):
        j = layer // 2
        if layer % 2 == 0:
            proj = _norm_matmul(x, norm_mix[layer], hgrn_w_in[j].astype(BF16))
            o_fwd = _hgrn_pass(proj, lbs[j, 0], None, None, starts, ends, reverse=False)
            gated = _hgrn_pass(proj, lbs[j, 1], o_fwd, hgrn_onorm[j], starts, ends, reverse=True)
            x = _matmul_residual(gated, hgrn_w_out[j].astype(BF16), x)
        else:
            proj = _norm_matmul(x, norm_mix[layer], conv_w_in[j].astype(BF16))
            x = _conv_out(proj, conv_w[j], conv_w_out[j].astype(BF16), x, starts, ends)
        x = _ec_moe(x, norm_ffn[layer], router_w[layer], w_gate[layer].astype(BF16),
                    w_up[layer].astype(BF16), w_down[layer].astype(BF16), group_sizes,
                    norm_final if layer == depth - 1 else None)
    outs, off = [], 0
    for g in groups:
        n = g.shape[0] * g.shape[1]
        outs.append(x[off:off + n].reshape(g.shape))
        off += n
    return tuple(outs)
```
